```python
import jax, jax.numpy as jnp
from jax import lax
import numpy as np

D_MODEL = 1024
BATCH = 16
SEQ = 2048
DEPTH = 2

GRID_W = 64
CTX_LEN = 256
N_AB = (DEPTH + 1) // 2
N_CD = DEPTH // 2

FNET_GROUP_DIM = 64
FNET_WIDTH = D_MODEL // 4
FNET_GROUPS = FNET_WIDTH // FNET_GROUP_DIM
RWKV_HEAD_DIM = 64
RWKV_WIDTH = D_MODEL - FNET_WIDTH
RWKV_HEADS = RWKV_WIDTH // RWKV_HEAD_DIM
RWKV_DECAY_RANK = 64
RWKV_AIC_RANK = 64
RWKV_GATE_RANK = 128
RWKV_SHIFT_WIDTH = 3 * RWKV_WIDTH + 2 * RWKV_DECAY_RANK + 2 * RWKV_AIC_RANK + RWKV_GATE_RANK
RWKV_GN_EPS = 64e-5
AB_IN = FNET_WIDTH + RWKV_SHIFT_WIDTH
RET_QK_DIM = 64
RET_V_DIM = 128
RET_V_WIDTH = 3 * D_MODEL // 4
RET_HEADS = RET_V_WIDTH // RET_V_DIM
RET_QK_WIDTH = RET_HEADS * RET_QK_DIM
RET_CHUNK = 128
RET_GN_EPS = 1e-5
ROPE_BASE = 10000.0
CONV_CH = D_MODEL - RET_V_WIDTH
CONV_WIDTH = 31
CD_IN = 2 * RET_QK_WIDTH + 2 * RET_V_WIDTH + 2 * CONV_CH
D_FF = ((8 * D_MODEL + 3 * 256 - 1) // (3 * 256)) * 256

kernel_name = 'hybrid_fnet_rwkv7_retnet_conformer_dit'


def rms_norm(x, g, eps=1e-6):
    xf = x.astype(jnp.float32)
    y = xf * lax.rsqrt(jnp.mean(jnp.square(xf), axis=-1, keepdims=True) + eps)
    return (y * g.astype(jnp.float32)).astype(x.dtype)


def layer_norm(x, g, b, eps=1e-5):
    xf = x.astype(jnp.float32)
    mu = jnp.mean(xf, axis=-1, keepdims=True)
    var = jnp.mean(jnp.square(xf - mu), axis=-1, keepdims=True)
    return (xf - mu) * lax.rsqrt(var + eps) * g + b


def head_group_norm(y, g, b, eps):
    mu = jnp.mean(y, axis=-1, keepdims=True)
    var = jnp.mean(jnp.square(y - mu), axis=-1, keepdims=True)
    yn = ((y - mu) * lax.rsqrt(var + eps)).reshape(y.shape[:2] + (-1,))
    return yn * g + b


def modulate(h, shift, scale):
    return h * (1.0 + scale) + shift


def split_cols(u, widths):
    return jnp.split(u, np.cumsum(widths)[:-1].tolist(), axis=-1)


def swiglu(h, w1, w3, w2):
    return (jax.nn.silu(h @ w1) * (h @ w3)) @ w2


def axial_rope(n_tokens, dim):
    rows = n_tokens // GRID_W
    row = jnp.repeat(jnp.arange(rows, dtype=jnp.float32), GRID_W)
    col = jnp.tile(jnp.arange(GRID_W, dtype=jnp.float32), rows)
    n_freq = dim // 4
    freqs = ROPE_BASE ** (-jnp.arange(n_freq, dtype=jnp.float32) / n_freq)
    ang = jnp.concatenate([row[:, None] * freqs, col[:, None] * freqs], axis=-1)
    return jnp.cos(ang), jnp.sin(ang)


def apply_rope(t, cos, sin):
    t1, t2 = jnp.split(t, 2, axis=-1)
    c = cos[None, :, None, :]
    s = sin[None, :, None, :]
    return jnp.concatenate([t1 * c - t2 * s, t1 * s + t2 * c], axis=-1)


def fourier_mix(u):
    b, l, _ = u.shape
    ug = u.astype(jnp.float32).reshape(b, l, FNET_GROUPS, FNET_GROUP_DIM)
    f = jnp.fft.fftn(ug, axes=(1, 3), norm='ortho').real
    return f.reshape(b, l, FNET_WIDTH)


def centred_shift(u, mu_prev, mu_next):
    zero = jnp.zeros_like(u[:, :1])
    prev = jnp.concatenate([zero, u[:, :-1]], axis=1)
    nxt = jnp.concatenate([u[:, 1:], zero], axis=1)
    return u + mu_prev * (prev - u) + mu_next * (nxt - u)


def rwkv7_features(u, mu_prev, mu_next, w0, w2, a0, a2, k_k, k_a):
    u = centred_shift(u, mu_prev, mu_next).astype(jnp.float32)
    r, k, v, wl_f, wl_b, al_f, al_b, gl = split_cols(u, [RWKV_WIDTH] * 3 + [RWKV_DECAY_RANK] * 2 + [RWKV_AIC_RANK] * 2 + [RWKV_GATE_RANK])
    shp = u.shape[:2] + (RWKV_HEADS, RWKV_HEAD_DIM)
    kk = (k * k_k).reshape(shp)
    kk = kk * lax.rsqrt(jnp.sum(jnp.square(kk), axis=-1, keepdims=True) + 1e-12)
    dirs = []
    for d, (wl, al) in enumerate(((wl_f, al_f), (wl_b, al_b))):
        wlog = -jax.nn.softplus(-(w0[d] + jnp.tanh(wl) @ w2[d])) - 0.5
        decay = jnp.exp(-jnp.exp(wlog))
        a = jax.nn.sigmoid(a0[d] + al @ a2[d])
        kd = k * (1.0 + (a - 1.0) * k_a)
        dirs.append((decay.reshape(shp), kd.reshape(shp), kk * a.reshape(shp)))
    return r.reshape(shp), v.reshape(shp), kk, dirs, gl


def rwkv7_scan(r, w, k, v, kk, b, state0, reverse, emit):
    def step(S, inp):
        r_t, w_t, k_t, v_t, kk_t, b_t = inp
        S_new = (S * w_t[:, :, None, :]
                 - jnp.einsum('bhvk,bhk->bhv', S, kk_t)[..., None] * b_t[:, :, None, :]
                 + v_t[..., None] * k_t[:, :, None, :])
        if not emit:
            return S_new, None
        y = jnp.einsum('bhvk,bhk->bhv', S if reverse else S_new, r_t)
        return S_new, y
    xs = tuple(jnp.swapaxes(t, 0, 1) for t in (r, w, k, v, kk, b))
    S_fin, ys = lax.scan(step, state0, xs, reverse=reverse)
    return (jnp.swapaxes(ys, 0, 1) if emit else None), S_fin


def rwkv7_output(y, r, k, v, gl, g2, r_k, lnx_g, lnx_b):
    b, l = y.shape[:2]
    yn = head_group_norm(y, lnx_g, lnx_b, RWKV_GN_EPS)
    bonus = (jnp.sum(r * k * r_k, axis=-1, keepdims=True) * v).reshape(b, l, RWKV_WIDTH)
    g = jax.nn.sigmoid(gl) @ g2
    return (yn + bonus) * g


def ab_mixer(hx, hc, w_in, mu_prev, mu_next, w0, w2, a0, a2, g2, k_k, k_a, r_k, lnx_g, lnx_b, ctx_out):
    px = hx @ w_in
    pc = hc @ (w_in if ctx_out else w_in[:, FNET_WIDTH:])
    rx, vx, kkx, dirs_x, glx = rwkv7_features(px[..., FNET_WIDTH:], mu_prev, mu_next, w0, w2, a0, a2, k_k, k_a)
    rc, vc, kkc, dirs_c, glc = rwkv7_features(pc[..., -RWKV_SHIFT_WIDTH:], mu_prev, mu_next, w0, w2, a0, a2, k_k, k_a)
    state0 = jnp.zeros((hx.shape[0], RWKV_HEADS, RWKV_HEAD_DIM, RWKV_HEAD_DIM), jnp.float32)
    ys_x, ys_c = [], []
    for d, reverse in enumerate((False, True)):
        dec_c, k_c, b_c = dirs_c[d]
        y_cd, S_ctx = rwkv7_scan(rc, dec_c, k_c, vc, kkc, b_c, state0, reverse, ctx_out)
        dec_x, k_x, b_x = dirs_x[d]
        y_xd, _ = rwkv7_scan(rx, dec_x, k_x, vx, kkx, b_x, S_ctx, reverse, True)
        ys_x.append(y_xd)
        ys_c.append(y_cd)
    out_x = jnp.concatenate([fourier_mix(px[..., :FNET_WIDTH]),
                             rwkv7_output(ys_x[0] + ys_x[1], rx, dirs_x[0][1], vx, glx, g2, r_k, lnx_g, lnx_b)], axis=-1)
    out_c = None
    if ctx_out:
        out_c = jnp.concatenate([fourier_mix(pc[..., :FNET_WIDTH]),
                                 rwkv7_output(ys_c[0] + ys_c[1], rc, dirs_c[0][1], vc, glc, g2, r_k, lnx_g, lnx_b)], axis=-1).astype(hc.dtype)
    return out_x.astype(hx.dtype), out_c


def retention_chunks(q, k, v, log_gamma, state0, exclusive, emit):
    b, l, h, _ = k.shape
    n = l // RET_CHUNK

    def chunks(t):
        return jnp.moveaxis(t.reshape(b, n, RET_CHUNK, h, t.shape[-1]), (1, 3), (0, 2))

    idx = jnp.arange(RET_CHUNK, dtype=jnp.float32)
    diff = idx[:, None] - idx[None, :]
    mask = diff > 0 if exclusive else diff >= 0
    inner_decay = jnp.where(mask, jnp.exp(log_gamma[:, None, None] * jnp.maximum(diff, 0.0)), 0.0)
    q_decay = jnp.exp(log_gamma[:, None] * (idx + 1.0))
    k_decay = jnp.exp(log_gamma[:, None] * (RET_CHUNK - 1.0 - idx))
    chunk_decay = jnp.exp(log_gamma * RET_CHUNK)

    def update(R, kb, vb):
        return R * chunk_decay[None, :, None, None] + jnp.einsum('bhjd,bhje->bhde', kb * k_decay[None, :, :, None], vb)

    if emit:
        def step(R, inp):
            qb, kb, vb = inp
            s = jnp.einsum('bhid,bhjd->bhij', qb, kb) * inner_decay[None]
            o = (jnp.einsum('bhij,bhje->bhie', s, vb)
                 + jnp.einsum('bhid,bhde->bhie', qb, R) * q_decay[None, :, :, None])
            return update(R, kb, vb), o
        R_fin, ys = lax.scan(step, state0, (chunks(q), chunks(k), chunks(v)))
        y = jnp.moveaxis(ys, (0, 2), (1, 3)).reshape(b, l, h, v.shape[-1])
        return y, R_fin

    def step_state(R, inp):
        kb, vb = inp
        return update(R, kb, vb), None
    R_fin, _ = lax.scan(step_state, state0, (chunks(k), chunks(v)))
    return None, R_fin


def retention_output(y, g, gn_g, gn_b):
    return head_group_norm(y, gn_g, gn_b, RET_GN_EPS) * jax.nn.silu(g.astype(jnp.float32))


def conformer_conv(u, conv_w, conv_b, ln_g, ln_b):
    u = u.astype(jnp.float32)
    a, gate = jnp.split(u, 2, axis=-1)
    h = a * jax.nn.sigmoid(gate)
    h = lax.conv_general_dilated(h, conv_w.astype(jnp.float32)[:, None, :], window_strides=(1,),
                                 padding=[(CONV_WIDTH // 2, CONV_WIDTH // 2)],
                                 dimension_numbers=('NWC', 'WIO', 'NWC'),
                                 feature_group_count=CONV_CH) + conv_b
    return jax.nn.silu(layer_norm(h, ln_g, ln_b))


def cd_mixer(hx, hc, rope_cos, rope_sin, w_in, decay_p, gn_g, gn_b, conv_w, conv_b, ln_g, ln_b, ctx_out):
    log_gamma = -jnp.exp(decay_p.astype(jnp.float32))
    widths = [RET_QK_WIDTH, RET_QK_WIDTH, RET_V_WIDTH, RET_V_WIDTH, 2 * CONV_CH]
    k_scale = RET_QK_DIM ** -0.5

    def heads(t, dim):
        return t.reshape(t.shape[:-1] + (-1, dim)).astype(jnp.float32)

    qx, kx, vx, gx, ux = split_cols(hx @ w_in, widths)
    qx = apply_rope(heads(qx, RET_QK_DIM), rope_cos, rope_sin)
    kx = apply_rope(heads(kx, RET_QK_DIM), rope_cos, rope_sin) * k_scale
    vx = heads(vx, RET_V_DIM)
    if ctx_out:
        qc, kc, vc, gc, uc = split_cols(hc @ w_in, widths)
        qc = heads(qc, RET_QK_DIM)
    else:
        kc, vc = split_cols(hc @ w_in[:, RET_QK_WIDTH:2 * RET_QK_WIDTH + RET_V_WIDTH], widths[1:3])
        qc = None
    kc = heads(kc, RET_QK_DIM) * k_scale
    vc = heads(vc, RET_V_DIM)

    def flip(t):
        return None if t is None else jnp.flip(t, axis=1)

    state0 = jnp.zeros((hx.shape[0], RET_HEADS, RET_QK_DIM, RET_V_DIM), jnp.float32)
    yc_f, R_f = retention_chunks(qc, kc, vc, log_gamma[0], state0, False, ctx_out)
    yx_f, _ = retention_chunks(qx, kx, vx, log_gamma[0], R_f, False, True)
    yc_b, R_b = retention_chunks(flip(qc), flip(kc), flip(vc), log_gamma[1], state0, True, ctx_out)
    yx_b, _ = retention_chunks(flip(qx), flip(kx), flip(vx), log_gamma[1], R_b, True, True)

    out_x = jnp.concatenate([retention_output(yx_f + flip(yx_b), gx, gn_g, gn_b),
                             conformer_conv(ux, conv_w, conv_b, ln_g, ln_b)], axis=-1)
    out_c = None
    if ctx_out:
        out_c = jnp.concatenate([retention_output(yc_f + flip(yc_b), gc, gn_g, gn_b),
                                 conformer_conv(uc, conv_w, conv_b, ln_g, ln_b)], axis=-1).astype(hc.dtype)
    return out_x.astype(hx.dtype), out_c


def setup_inputs(seed: int = 0) -> dict:
    key = jax.random.key(seed)
    ks = iter(jax.random.split(key, 48))

    def nrm(shape, scale):
        return scale * jax.random.normal(next(ks), shape, jnp.float32)

    def gain(shape):
        return 1.0 + nrm(shape, 0.02)

    def unif(shape, lo, hi):
        return jax.random.uniform(next(ks), shape, jnp.float32, lo, hi)

    D = D_MODEL
    W = RWKV_WIDTH
    gam = 1.0 - 2.0 ** (-5.0 - np.arange(RET_HEADS))
    p0 = jnp.asarray(np.log(-np.log(gam)).astype(np.float32))
    return {
        'x': nrm((BATCH, SEQ, D), 1.0),
        'c': nrm((BATCH, D), 1.0),
        'ctx': nrm((BATCH, CTX_LEN, D), 1.0),
        'c_ctx': nrm((D,), 1.0),
        'ada_w': nrm((DEPTH, D, 6 * D), 0.5 * D ** -0.5),
        'ada_b': nrm((DEPTH, 6 * D), 0.01),
        'norm_mix_g': gain((DEPTH, D)),
        'norm_ffn_g': gain((DEPTH, D)),
        'ab_w_in': nrm((N_AB, D, AB_IN), D ** -0.5),
        'ab_mu_prev': unif((N_AB, RWKV_SHIFT_WIDTH), 0.0, 0.5),
        'ab_mu_next': unif((N_AB, RWKV_SHIFT_WIDTH), 0.0, 0.5),
        'ab_w0': unif((N_AB, 2, W), -4.0, 0.0),
        'ab_w2': nrm((N_AB, 2, RWKV_DECAY_RANK, W), 0.1 * RWKV_DECAY_RANK ** -0.5),
        'ab_a0': nrm((N_AB, 2, W), 0.1),
        'ab_a2': nrm((N_AB, 2, RWKV_AIC_RANK, W), 0.1 * RWKV_AIC_RANK ** -0.5),
        'ab_g2': nrm((N_AB, RWKV_GATE_RANK, W), RWKV_GATE_RANK ** -0.5),
        'ab_k_k': 0.85 + nrm((N_AB, W), 0.02),
        'ab_k_a': gain((N_AB, W)),
        'ab_r_k': nrm((N_AB, RWKV_HEADS, RWKV_HEAD_DIM), 0.1),
        'ab_lnx_g': gain((N_AB, W)),
        'ab_lnx_b': nrm((N_AB, W), 0.01),
        'ab_w_out': nrm((N_AB, D, D), D ** -0.5),
        'cd_w_in': nrm((N_CD, D, CD_IN), D ** -0.5),
        'cd_decay_p': p0 + nrm((N_CD, 2, RET_HEADS), 0.05),
        'cd_gn_g': gain((N_CD, RET_V_WIDTH)),
        'cd_gn_b': nrm((N_CD, RET_V_WIDTH), 0.01),
        'cd_conv_w': nrm((N_CD, CONV_WIDTH, CONV_CH), CONV_WIDTH ** -0.5),
        'cd_conv_b': nrm((N_CD, CONV_CH), 0.01),
        'cd_ln_g': gain((N_CD, CONV_CH)),
        'cd_ln_b': nrm((N_CD, CONV_CH), 0.01),
        'cd_w_out': nrm((N_CD, D, D), D ** -0.5),
        'ffn_w1': nrm((DEPTH, D, D_FF), D ** -0.5),
        'ffn_w3': nrm((DEPTH, D, D_FF), D ** -0.5),
        'ffn_w2': nrm((DEPTH, D_FF, D), D_FF ** -0.5),
        'final_g': gain((D,)),
    }


def reference(x, c, ctx, c_ctx, ada_w, ada_b, norm_mix_g, norm_ffn_g,
              ab_w_in, ab_mu_prev, ab_mu_next, ab_w0, ab_w2, ab_a0, ab_a2, ab_g2, ab_k_k, ab_k_a, ab_r_k,
              ab_lnx_g, ab_lnx_b, ab_w_out,
              cd_w_in, cd_decay_p, cd_gn_g, cd_gn_b, cd_conv_w, cd_conv_b, cd_ln_g, cd_ln_b, cd_w_out,
              ffn_w1, ffn_w3, ffn_w2, final_g):
    rope_cos, rope_sin = axial_rope(x.shape[1], RET_QK_DIM)
    lat, con = x, ctx
    for i in range(DEPTH):
        last = i == DEPTH - 1
        j = i // 2
        mod_x = [m[:, None, :] for m in jnp.split(jax.nn.silu(c) @ ada_w[i] + ada_b[i], 6, axis=-1)]
        mod_c = jnp.split(jax.nn.silu(c_ctx) @ ada_w[i] + ada_b[i], 6, axis=-1)
        hx = modulate(rms_norm(lat, norm_mix_g[i]), mod_x[0], mod_x[1])
        hc = modulate(rms_norm(con, norm_mix_g[i]), mod_c[0], mod_c[1])
        if i % 2 == 0:
            mx, mc = ab_mixer(hx, hc, ab_w_in[j], ab_mu_prev[j], ab_mu_next[j], ab_w0[j], ab_w2[j], ab_a0[j], ab_a2[j],
                              ab_g2[j], ab_k_k[j], ab_k_a[j], ab_r_k[j], ab_lnx_g[j], ab_lnx_b[j], not last)
            w_out = ab_w_out[j]
        else:
            mx, mc = cd_mixer(hx, hc, rope_cos, rope_sin, cd_w_in[j], cd_decay_p[j], cd_gn_g[j], cd_gn_b[j],
                              cd_conv_w[j], cd_conv_b[j], cd_ln_g[j], cd_ln_b[j], not last)
            w_out = cd_w_out[j]
        lat = lat + mod_x[2] * (mx @ w_out)
        lat = lat + mod_x[5] * swiglu(modulate(rms_norm(lat, norm_ffn_g[i]), mod_x[3], mod_x[4]),
                                      ffn_w1[i], ffn_w3[i], ffn_w2[i])
        if not last:
            con = con + mod_c[2] * (mc @ w_out)
            con = con + mod_c[5] * swiglu(modulate(rms_norm(con, norm_ffn_g[i]), mod_c[3], mod_c[4]),
                                          ffn_w1[i], ffn_w3[i], ffn_w2[i])
    return rms_norm(lat, final_g)
```

```python
import functools
import math

import jax
import jax.numpy as jnp
from jax import lax
from jax.experimental import pallas as pl
from jax.experimental.pallas import tpu as pltpu

F32 = jnp.float32
BF16 = jnp.bfloat16

GRID_W = 64
FNET_GROUP_DIM = 64
RWKV_HEAD_DIM = 64
RWKV_DECAY_RANK = 64
RWKV_AIC_RANK = 64
RWKV_GATE_RANK = 128
RWKV_GN_EPS = 64e-5
RET_QK_DIM = 64
RET_V_DIM = 128
RET_CHUNK = 128
RET_GN_EPS = 1e-5
ROPE_BASE = 10000.0
CONV_WIDTH = 31
RMS_EPS = 1e-6
LN_EPS = 1e-5

ROW_TILE = 256
RWKV_CHUNK = 64
HALO = 16
VMEM_LIMIT = 56 * 1024 * 1024


def _cparams(sem):
    return pltpu.CompilerParams(dimension_semantics=sem, vmem_limit_bytes=VMEM_LIMIT)


_NN = (((1,), (0,)), ((), ()))
_NT = (((1,), (1,)), ((), ()))


def _sigmoid(x):
    return 1.0 / (1.0 + jnp.exp(-x))


def _silu(x):
    return x * _sigmoid(x)


def _split2(x):
    hi = x.astype(BF16)
    lo = (x - hi.astype(F32)).astype(BF16)
    return hi, lo


def _dot1(a, b, dims=_NN):
    return lax.dot_general(a.astype(BF16), b.astype(BF16), dims, preferred_element_type=F32)


def _dot3(a, b, dims=_NN):
    ah, al = _split2(a)
    bh, bl = _split2(b)
    d = lambda x, y: lax.dot_general(x, y, dims, preferred_element_type=F32)
    return d(ah, bh) + (d(ah, bl) + d(al, bh))


def _dot_exact_lhs(a_bf16, b):
    b1 = b.astype(BF16)
    r1 = b - b1.astype(F32)
    b2 = r1.astype(BF16)
    b3 = (r1 - b2.astype(F32)).astype(BF16)
    d = lambda y: lax.dot_general(a_bf16, y, _NN, preferred_element_type=F32)
    return d(b1) + (d(b2) + d(b3))


def _dot_exact_rhs(a, b_bf16):
    ah, al = _split2(a)
    d = lambda x: lax.dot_general(x, b_bf16, _NN, preferred_element_type=F32)
    return d(ah) + d(al)


def _iota(shape, axis):
    return lax.broadcasted_iota(jnp.int32, shape, axis)


def _const_spec(shape):
    n = len(shape)
    return pl.BlockSpec(shape, lambda *_: (0,) * n)


def _mods_kernel(c_ref, w_ref, b_ref, o_ref):
    o_ref[0] = _dot3(_silu(c_ref[...]), w_ref[0]) + b_ref[0]


def _mods(cpad, ada_w, ada_b):
    depth, d, n = ada_w.shape
    tn = n // 4
    return pl.pallas_call(
        _mods_kernel,
        grid=(depth, n // tn),
        in_specs=[pl.BlockSpec(cpad.shape, lambda l, j: (0, 0)),
                  pl.BlockSpec((1, d, tn), lambda l, j: (l, 0, j)),
                  pl.BlockSpec((1, 1, tn), lambda l, j: (l, 0, j))],
        out_specs=pl.BlockSpec((1, cpad.shape[0], tn), lambda l, j: (l, 0, j)),
        out_shape=jax.ShapeDtypeStruct((depth, cpad.shape[0], n), F32),
        compiler_params=_cparams(("parallel", "parallel")),
        name="adaln_mods",
    )(cpad, ada_w, ada_b.reshape(depth, 1, n))


def _rms_mod(x, g, shift, scale):
    y = x * lax.rsqrt(jnp.mean(x * x, axis=-1, keepdims=True) + RMS_EPS) * g
    return y * (1.0 + scale) + shift


def _nm_matmul_kernel(x_ref, mod_ref, g_ref, w_ref, *out_refs, splits):
    d = x_ref.shape[-1]
    mod = mod_ref[0]
    h = _rms_mod(x_ref[...], g_ref[...], mod[:, 0:d], mod[:, d:2 * d]).astype(BF16)
    off = 0
    for o_ref, wd in zip(out_refs, splits):
        o_ref[...] = jnp.dot(h, w_ref[:, off:off + wd], preferred_element_type=F32)
        off += wd


def _nm_matmul(stream, mods_tab, mod_row, g, w_bf16, splits):
    rows, d = stream.shape
    n = w_bf16.shape[1]
    assert sum(splits) == n
    return pl.pallas_call(
        functools.partial(_nm_matmul_kernel, splits=splits),
        grid=(rows // ROW_TILE,),
        in_specs=[pl.BlockSpec((ROW_TILE, d), lambda i: (i, 0)),
                  pl.BlockSpec((1, 1, mods_tab.shape[-1]), lambda i: (mod_row(i), 0, 0)),
                  _const_spec((1, d)),
                  _const_spec((d, n))],
        out_specs=[pl.BlockSpec((ROW_TILE, wd), lambda i: (i, 0)) for wd in splits],
        out_shape=[jax.ShapeDtypeStruct((rows, wd), F32) for wd in splits],
        compiler_params=_cparams(("parallel",)),
        name="norm_mod_proj",
    )(stream, mods_tab, g.reshape(1, d), w_bf16)


def _outproj_ffn_kernel(*refs, n_parts, final):
    lat_ref = refs[0]
    part_refs = refs[1:1 + n_parts]
    wo_refs = refs[1 + n_parts:1 + 2 * n_parts]
    mod_ref, gf_ref, w1_ref, w3_ref, w2_ref, fg_ref, o_ref = refs[1 + 2 * n_parts:]
    d = lat_ref.shape[-1]
    mod = mod_ref[0]
    acc = None
    for p_ref, wo_ref in zip(part_refs, wo_refs):
        t = jnp.dot(p_ref[...].astype(BF16), wo_ref[...], preferred_element_type=F32)
        acc = t if acc is None else acc + t
    lat = lat_ref[...] + mod[:, 2 * d:3 * d] * acc
    h = _rms_mod(lat, gf_ref[...], mod[:, 3 * d:4 * d], mod[:, 4 * d:5 * d]).astype(BF16)
    a = jnp.dot(h, w1_ref[...], preferred_element_type=F32)
    b = jnp.dot(h, w3_ref[...], preferred_element_type=F32)
    act = (_silu(a) * b).astype(BF16)
    out = lat + mod[:, 5 * d:6 * d] * jnp.dot(act, w2_ref[...], preferred_element_type=F32)
    if final:
        out = out * lax.rsqrt(jnp.mean(out * out, axis=-1, keepdims=True) + RMS_EPS) * fg_ref[...]
    o_ref[...] = out


def _outproj_ffn(stream, n_out_tiles, lat_tile, parts, part_tile, w_out_bf16, mods_tab, mod_row,
                 gf, w1, w3, w2, final_g, final):
    d = stream.shape[-1]
    dff = w1.shape[1]
    widths = [p.shape[1] for p in parts]
    offs = [sum(widths[:k]) for k in range(len(widths))]
    wo_parts = [w_out_bf16[o:o + wd] for o, wd in zip(offs, widths)]
    in_specs = [pl.BlockSpec((ROW_TILE, d), lambda i: (lat_tile(i), 0))]
    in_specs += [pl.BlockSpec((ROW_TILE, wd), lambda i: (part_tile(i), 0)) for wd in widths]
    in_specs += [_const_spec((wd, d)) for wd in widths]
    in_specs += [pl.BlockSpec((1, 1, mods_tab.shape[-1]), lambda i: (mod_row(i), 0, 0)),
                 _const_spec((1, d)), _const_spec((d, dff)), _const_spec((d, dff)),
                 _const_spec((dff, d)), _const_spec((1, d))]
    return pl.pallas_call(
        functools.partial(_outproj_ffn_kernel, n_parts=len(parts), final=final),
        grid=(n_out_tiles,),
        in_specs=in_specs,
        out_specs=pl.BlockSpec((ROW_TILE, d), lambda i: (i, 0)),
        out_shape=jax.ShapeDtypeStruct((n_out_tiles * ROW_TILE, d), F32),
        compiler_params=_cparams(("parallel",)),
        name="outproj_ffn",
    )(stream, *parts, *wo_parts, mods_tab, gf.reshape(1, d), w1, w3, w2, final_g.reshape(1, d))


def _fourier_kernel(u_ref, bcs_ref, cls_ref, o_ref, ucs_ref):
    @pl.when(pl.program_id(1) == 0)
    def _():
        w = u_ref.shape[-1]
        ub = jnp.dot(u_ref[0].astype(BF16), bcs_ref[...], preferred_element_type=F32)
        ucs_ref[...] = jnp.concatenate([ub[:, :w], ub[:, w:]], axis=0).astype(BF16)
    o_ref[0] = jnp.dot(cls_ref[...], ucs_ref[...], preferred_element_type=F32)


def _dft_tables(n):
    j = jnp.arange(n, dtype=jnp.int32)
    jk = (j[:, None] * j[None, :]) % n
    ang = jk.astype(F32) * (2.0 * math.pi / n)
    return jnp.cos(ang), jnp.sin(ang)


def _fourier(u):
    b, l, w = u.shape
    gc, gs = _dft_tables(FNET_GROUP_DIM)
    eye = jnp.eye(w // FNET_GROUP_DIM, dtype=F32)
    scale = 1.0 / math.sqrt(l * FNET_GROUP_DIM)
    bcs = jnp.concatenate([jnp.kron(eye, gc), jnp.kron(eye, gs)], axis=1).astype(BF16)
    cl, sl = _dft_tables(l)
    cls = (jnp.concatenate([cl, -sl], axis=1) * scale).astype(BF16)
    tl = min(l, ROW_TILE)
    return pl.pallas_call(
        _fourier_kernel,
        grid=(b, l // tl),
        in_specs=[pl.BlockSpec((1, l, w), lambda i, j: (i, 0, 0)),
                  _const_spec((w, 2 * w)),
                  pl.BlockSpec((tl, 2 * l), lambda i, j: (j, 0))],
        out_specs=pl.BlockSpec((1, tl, w), lambda i, j: (i, j, 0)),
        out_shape=jax.ShapeDtypeStruct((b, l, w), F32),
        scratch_shapes=[pltpu.VMEM((2 * l, w), BF16)],
        compiler_params=_cparams(("parallel", "arbitrary")),
        name="fourier_mix",
    )(u, bcs, cls)


def _rwkv_feat_kernel(p_ref, pp_ref, pn_ref, mu_p_ref, mu_n_ref, w0_ref, w2_ref, a0_ref, a2_ref, g2_ref,
                      kk_w_ref, ka_w_ref, rk_w_ref, e_ref,
                      r_ref, v_ref, kk_ref, g_ref, bonus_ref, lw_ref, kd_ref, bb_ref,
                      *, tiles_per_seq, ctx_tiles):
    i = pl.program_id(0)
    j = i % tiles_per_seq
    has_prev = jnp.logical_and(j != 0, j != ctx_tiles).astype(F32)
    has_next = jnp.logical_and(j != ctx_tiles - 1, j != tiles_per_seq - 1).astype(F32)
    u = p_ref[...]
    tm = u.shape[0]
    row = _iota(u.shape, 0)
    prev = jnp.where(row == 0, pp_ref[7:8, :] * has_prev, pltpu.roll(u, 1, 0))
    nxt = jnp.where(row == tm - 1, pn_ref[0:1, :] * has_next, pltpu.roll(u, tm - 1, 0))
    us = u + mu_p_ref[...] * (prev - u) + mu_n_ref[...] * (nxt - u)
    w = r_ref.shape[-1]
    r = us[:, 0:w]
    k = us[:, w:2 * w]
    v = us[:, 2 * w:3 * w]
    o = 3 * w
    wl = us[:, o:o + 2 * RWKV_DECAY_RANK]
    o += 2 * RWKV_DECAY_RANK
    al = us[:, o:o + 2 * RWKV_AIC_RANK]
    o += 2 * RWKV_AIC_RANK
    gl = us[:, o:o + RWKV_GATE_RANK]
    e = e_ref[...]
    kk = k * kk_w_ref[...]
    kk = kk * lax.rsqrt(_dot_exact_rhs(kk * kk, e) + 1e-12)
    z = w0_ref[...] + _dot3(jnp.tanh(wl), w2_ref[...])
    wlog = -(jnp.maximum(-z, 0.0) + jnp.log(1.0 + jnp.exp(-jnp.abs(z)))) - 0.5
    lw_ref[...] = -jnp.exp(wlog)
    a = _sigmoid(a0_ref[...] + _dot3(al, a2_ref[...]))
    ka_w = ka_w_ref[...]
    kd_f = k * (1.0 + (a[:, :w] - 1.0) * ka_w)
    kd_ref[:, :w] = kd_f
    kd_ref[:, w:] = k * (1.0 + (a[:, w:] - 1.0) * ka_w)
    bb_ref[:, :w] = kk * a[:, :w]
    bb_ref[:, w:] = kk * a[:, w:]
    r_ref[...] = r
    v_ref[...] = v
    kk_ref[...] = kk
    g_ref[...] = _dot3(_sigmoid(gl), g2_ref[...])
    bonus_ref[...] = _dot_exact_rhs(r * kd_f * rk_w_ref[...], e) * v


def _rwkv_features(pr, tiles_per_seq, ctx_tiles, mu_prev, mu_next, w0, w2, a0, a2, g2, k_k, k_a, r_k):
    rows, sw = pr.shape
    w = k_k.shape[0]
    heads = w // RWKV_HEAD_DIM
    n_tiles = rows // ROW_TILE
    sub = ROW_TILE // 8
    zeros = jnp.zeros((RWKV_DECAY_RANK, w), F32)
    w2bd = jnp.concatenate([jnp.concatenate([w2[0], zeros], 1), jnp.concatenate([zeros, w2[1]], 1)], 0)
    a2bd = jnp.concatenate([jnp.concatenate([a2[0], zeros], 1), jnp.concatenate([zeros, a2[1]], 1)], 0)
    e = jnp.kron(jnp.eye(heads, dtype=F32), jnp.ones((RWKV_HEAD_DIM, RWKV_HEAD_DIM), F32)).astype(BF16)
    row1 = lambda x: x.reshape(1, -1)
    tile_w = pl.BlockSpec((ROW_TILE, w), lambda i: (i, 0))
    tile_2w = pl.BlockSpec((ROW_TILE, 2 * w), lambda i: (i, 0))
    return pl.pallas_call(
        functools.partial(_rwkv_feat_kernel, tiles_per_seq=tiles_per_seq, ctx_tiles=ctx_tiles),
        grid=(n_tiles,),
        in_specs=[pl.BlockSpec((ROW_TILE, sw), lambda i: (i, 0)),
                  pl.BlockSpec((8, sw), lambda i: (jnp.maximum(i * sub - 1, 0), 0)),
                  pl.BlockSpec((8, sw), lambda i: (jnp.minimum((i + 1) * sub, n_tiles * sub - 1), 0)),
                  _const_spec((1, sw)), _const_spec((1, sw)),
                  _const_spec((1, 2 * w)), _const_spec((2 * RWKV_DECAY_RANK, 2 * w)),
                  _const_spec((1, 2 * w)), _const_spec((2 * RWKV_AIC_RANK, 2 * w)),
                  _const_spec((RWKV_GATE_RANK, w)),
                  _const_spec((1, w)), _const_spec((1, w)), _const_spec((1, w)), _const_spec((w, w))],
        out_specs=[tile_w] * 5 + [tile_2w] * 3,
        out_shape=[jax.ShapeDtypeStruct((rows, w), F32)] * 5 + [jax.ShapeDtypeStruct((rows, 2 * w), F32)] * 3,
        compiler_params=_cparams(("parallel",)),
        name="rwkv_features",
    )(pr, pr, pr, row1(mu_prev), row1(mu_next), row1(w0), w2bd, row1(a0), a2bd, g2,
      row1(k_k), row1(k_a), row1(r_k), e)


def _rwkv_chunk_dir(r, v, kk, lw, kd, bb, h0, rev):
    c = r.shape[0]
    n = 2 * c
    ti = _iota((c, c), 0)
    si = _iota((c, c), 1)
    tri = jnp.where((si >= ti) if rev else (si <= ti), 1.0, 0.0).astype(BF16)
    cum = _dot_exact_lhs(tri, lw)
    cum_prev = cum - lw
    total = cum[0:1, :] if rev else cum[c - 1:c, :]
    e_prev = jnp.exp(cum_prev)
    e_neg = jnp.exp(-cum)
    e_tail = jnp.exp(total - cum)
    a_t = -kk * e_prev
    r_t = r * (e_prev if rev else jnp.exp(cum))
    b_t = bb * e_neg
    k_t = kd * e_neg
    b_h = bb * e_tail
    k_h = kd * e_tail
    g_c = jnp.exp(total)

    m0 = _iota((c, 2 * RWKV_HEAD_DIM), 1) < RWKV_HEAD_DIM
    stack2 = lambda x: jnp.concatenate([jnp.where(m0, x, 0.0), jnp.where(m0, 0.0, x)], axis=0)
    dup = lambda x: jnp.concatenate([x, x], axis=0)
    a_m = stack2(a_t)
    gram = _dot3(jnp.concatenate([a_m, stack2(r_t)], axis=0),
                 jnp.concatenate([dup(b_t), dup(k_t)], axis=0), _NT)

    i = _iota((n, n), 0)
    j = _iota((n, n), 1)
    same = (i >= c) == (j >= c)
    it = i & (c - 1)
    jt = j & (c - 1)
    before = (jt > it) if rev else (jt < it)
    strict = jnp.logical_and(same, before)
    rmask = strict if rev else jnp.logical_and(same, jnp.logical_or(before, jt == it))
    eye = i == j
    a_ab = jnp.where(strict, gram[:n, :n], 0.0)
    a_ak = jnp.where(strict, gram[:n, n:], 0.0)
    a_rb = jnp.where(rmask, gram[n:, :n], 0.0)
    a_rk = jnp.where(rmask, gram[n:, n:], 0.0)

    p = jnp.where(eye, 1.0, 0.0) + a_ab
    ak = _dot3(a_ab, a_ab)
    steps = int(math.log2(c)) - 2
    for _ in range(steps):
        x = _dot3(jnp.concatenate([p, ak], axis=0), ak)
        p = p + x[:n]
        ak = x[n:]
    p = p + _dot3(p, ak)

    v_m = stack2(v)
    av = _dot3(p, jnp.concatenate([a_m, _dot3(a_ak, v_m)], axis=1))
    x1 = _dot3(jnp.concatenate([a_rb, stack2(b_h).T], axis=0), av)
    x2 = _dot3(jnp.concatenate([a_rk, stack2(k_h).T], axis=0), v_m)
    r_rows = stack2(r_t) + x1[:n, :n]
    y_rows = _dot3(r_rows, h0) + (x1[:n, n:] + x2[:n])
    y = y_rows[:c] + y_rows[c:]
    m = jnp.where(eye, jnp.broadcast_to(g_c, (n, n)), 0.0) + x1[n:, :n]
    h1 = _dot3(m, h0) + (x1[n:, n:] + x2[n:])
    return y, h1


def _rwkv_chunk_kernel(rf, vf, kkf, lwf, kdf, bbf, rb, vb, kkb, lwb, kdb, bbb, yf_ref, yb_ref, h_ref):
    @pl.when(pl.program_id(2) == 0)
    def _():
        h_ref[...] = jnp.zeros_like(h_ref)
    y, h1 = _rwkv_chunk_dir(rf[...], vf[...], kkf[...], lwf[...], kdf[...], bbf[...], h_ref[0], False)
    yf_ref[...] = y
    h_ref[0] = h1
    y, h1 = _rwkv_chunk_dir(rb[...], vb[...], kkb[...], lwb[...], kdb[...], bbb[...], h_ref[1], True)
    yb_ref[...] = y
    h_ref[1] = h1


def _rwkv_chunks(r, v, kk, lw, kd, bb, batch, n_chunks, ctx_chunks):
    rows, w = r.shape
    c = RWKV_CHUNK
    pairs = w // (2 * RWKV_HEAD_DIM)
    lanes = 2 * RWKV_HEAD_DIM
    fwd = lambda s: s
    bwd = lambda s: jnp.where(s < ctx_chunks, ctx_chunks - 1 - s, n_chunks + ctx_chunks - 1 - s)

    def spec(order, lane_off):
        return pl.BlockSpec((c, lanes), lambda b, p, s: (b * n_chunks + order(s), lane_off + p))

    return pl.pallas_call(
        _rwkv_chunk_kernel,
        grid=(batch, pairs, n_chunks),
        in_specs=[spec(fwd, 0)] * 6 + [spec(bwd, 0)] * 3 + [spec(bwd, pairs)] * 3,
        out_specs=[spec(fwd, 0), spec(bwd, 0)],
        out_shape=[jax.ShapeDtypeStruct((rows, w), F32)] * 2,
        scratch_shapes=[pltpu.VMEM((2, lanes, lanes), F32)],
        compiler_params=_cparams(("parallel", "parallel", "arbitrary")),
        name="rwkv_chunks",
    )(r, v, kk, lw, kd, bb, r, v, kk, lw, kd, bb)


def _rwkv_out_kernel(yf_ref, yb_ref, bonus_ref, g_ref, lg_ref, lb_ref, e_ref, o_ref):
    e = e_ref[...]
    inv = 1.0 / RWKV_HEAD_DIM
    y = yf_ref[...] + yb_ref[...]
    dlt = y - _dot_exact_rhs(y, e) * inv
    var = _dot_exact_rhs(dlt * dlt, e) * inv
    yn = dlt * lax.rsqrt(var + RWKV_GN_EPS) * lg_ref[...] + lb_ref[...]
    o_ref[...] = (yn + bonus_ref[...]) * g_ref[...]


def _rwkv_out(yf, yb, bonus, g, lnx_g, lnx_b):
    rows, w = yf.shape
    heads = w // RWKV_HEAD_DIM
    e = jnp.kron(jnp.eye(heads, dtype=F32), jnp.ones((RWKV_HEAD_DIM, RWKV_HEAD_DIM), F32)).astype(BF16)
    tile = pl.BlockSpec((ROW_TILE, w), lambda i: (i, 0))
    return pl.pallas_call(
        _rwkv_out_kernel,
        grid=(rows // ROW_TILE,),
        in_specs=[tile] * 4 + [_const_spec((1, w))] * 2 + [_const_spec((w, w))],
        out_specs=tile,
        out_shape=jax.ShapeDtypeStruct((rows, w), F32),
        compiler_params=_cparams(("parallel",)),
        name="rwkv_out",
    )(yf, yb, bonus, g, lnx_g.reshape(1, w), lnx_b.reshape(1, w), e)


def _pair_decay_log(dp_ref, d, p, shape, axis):
    half = shape[axis] // 2
    first = _iota(shape, axis) < half
    val = jnp.where(first, dp_ref[d, 2 * p], dp_ref[d, 2 * p + 1])
    return -jnp.exp(jnp.full(shape, 0.0, F32) + val)


def _rope(x_ref, xp_ref, cos_ref, sin_ref):
    return x_ref[...] * cos_ref[...] + xp_ref[...] * sin_ref[...]


def _ret_state_kernel(dp_ref, kf, kpf, cf, sf, vf, kb, kpb, cb, sb, vb, rf_ref, rb_ref, st_ref):
    p = pl.program_id(1)

    @pl.when(pl.program_id(2) == 0)
    def _():
        st_ref[...] = jnp.zeros_like(st_ref)

    c = kf.shape[0]
    lanes = kf.shape[1]
    vd = vf.shape[1] // 2
    pos = _iota((c, lanes), 0).astype(F32)
    m0 = _iota((c, lanes), 1) < lanes // 2
    for d, (k_ref, kp_ref, c_ref, s_ref, v_ref, out_ref) in enumerate(
            ((kf, kpf, cf, sf, vf, rf_ref), (kb, kpb, cb, sb, vb, rb_ref))):
        st = st_ref[d]
        out_ref[0, 0, 0] = st
        lg = _pair_decay_log(dp_ref, d, p, (c, lanes), 1)
        dist = (c - 1.0 - pos) if d == 0 else pos
        kd = _rope(k_ref, kp_ref, c_ref, s_ref) * (RET_QK_DIM ** -0.5) * jnp.exp(lg * dist)
        v = v_ref[...]
        upd = (_dot3(jnp.where(m0, kd, 0.0).T, v[:, :vd]) + _dot3(jnp.where(m0, 0.0, kd).T, v[:, vd:]))
        lg_rows = _pair_decay_log(dp_ref, d, p, (lanes, vd), 0)
        st_ref[d] = st * jnp.exp(lg_rows * c) + upd


def _ret_states(decay_p, k, kp, cos, sin, v, batch, n_chunks, ctx_chunks):
    c = RET_CHUNK
    qk_w = k.shape[1] // 2
    pairs = qk_w // (2 * RET_QK_DIM)
    lanes = 2 * RET_QK_DIM
    fwd = lambda s: s
    bwd = lambda s: jnp.where(s < ctx_chunks, ctx_chunks - 1 - s, n_chunks + ctx_chunks - 1 - s)

    def in_specs(order):
        row = lambda b, s: b * n_chunks + order(s)
        return [pl.BlockSpec((c, lanes), lambda b, p, s: (row(b, s), pairs + p)),
                pl.BlockSpec((c, lanes), lambda b, p, s: (row(b, s), pairs + p)),
                pl.BlockSpec((c, lanes), lambda b, p, s: (order(s), 0)),
                pl.BlockSpec((c, lanes), lambda b, p, s: (order(s), 0)),
                pl.BlockSpec((c, 2 * RET_V_DIM), lambda b, p, s: (row(b, s), p))]

    def out_spec(order):
        return pl.BlockSpec((1, 1, 1, lanes, RET_V_DIM), lambda b, p, s: (b, p, order(s), 0, 0))

    shape = jax.ShapeDtypeStruct((batch, pairs, n_chunks, lanes, RET_V_DIM), F32)
    return pl.pallas_call(
        _ret_state_kernel,
        grid=(batch, pairs, n_chunks),
        in_specs=[pl.BlockSpec(memory_space=pltpu.SMEM)] + in_specs(fwd) + in_specs(bwd),
        out_specs=[out_spec(fwd), out_spec(bwd)],
        out_shape=[shape, shape],
        scratch_shapes=[pltpu.VMEM((2, lanes, RET_V_DIM), F32)],
        compiler_params=_cparams(("parallel", "parallel", "arbitrary")),
        name="retention_states",
    )(decay_p, k, kp, cos, sin, v, k, kp, cos, sin, v)


def _ret_out_kernel(dp_ref, q_ref, qp_ref, k_ref, kp_ref, cos_ref, sin_ref, v_ref, g_ref, rf_ref, rb_ref,
                    gg_ref, gb_ref, o_ref):
    p = pl.program_id(1)
    c, lanes = q_ref.shape
    vd = RET_V_DIM
    assert c == vd
    q = _rope(q_ref, qp_ref, cos_ref, sin_ref)
    k = _rope(k_ref, kp_ref, cos_ref, sin_ref) * (RET_QK_DIM ** -0.5)
    m0 = _iota((c, lanes), 1) < lanes // 2
    ai = _iota((c, c), 0)
    bi = _iota((c, c), 1)
    diff = (ai - bi).astype(F32)
    pos1 = (_iota((c, vd), 0) + 1).astype(F32)
    for h in range(2):
        qh = jnp.where(m0, q, 0.0) if h == 0 else jnp.where(m0, 0.0, q)
        lg_f = -jnp.exp(jnp.full((c, c), 0.0, F32) + dp_ref[0, 2 * p + h])
        lg_b = -jnp.exp(jnp.full((c, c), 0.0, F32) + dp_ref[1, 2 * p + h])
        decay = jnp.where(bi <= ai, jnp.exp(lg_f * jnp.maximum(diff, 0.0)), jnp.exp(lg_b * jnp.maximum(-diff, 0.0)))
        s = _dot3(qh, k, _NT) * decay
        vh = v_ref[:, h * vd:(h + 1) * vd]
        qf = jnp.exp(lg_f * pos1)
        qb = jnp.exp(lg_b * (c + 1.0 - pos1))
        y = _dot3(s, vh) + _dot3(qh, rf_ref[0, 0, 0]) * qf + _dot3(qh, rb_ref[0, 0, 0]) * qb
        mu = jnp.mean(y, axis=-1, keepdims=True)
        dlt = y - mu
        var = jnp.mean(dlt * dlt, axis=-1, keepdims=True)
        yn = dlt * lax.rsqrt(var + RET_GN_EPS) * gg_ref[:, h * vd:(h + 1) * vd] + gb_ref[:, h * vd:(h + 1) * vd]
        o_ref[:, h * vd:(h + 1) * vd] = yn * _silu(g_ref[:, h * vd:(h + 1) * vd])


def _ret_out(decay_p, qk, qkp, cos, sin, v, g, rf, rb, gn_g, gn_b, batch, n_chunks, ctx_chunks):
    c = RET_CHUNK
    qk_w = qk.shape[1] // 2
    pairs = qk_w // (2 * RET_QK_DIM)
    lanes = 2 * RET_QK_DIM
    x_chunks = n_chunks - ctx_chunks
    vw = v.shape[1]
    row = lambda b, s: b * n_chunks + ctx_chunks + s
    qspec = pl.BlockSpec((c, lanes), lambda b, p, s: (row(b, s), p))
    kspec = pl.BlockSpec((c, lanes), lambda b, p, s: (row(b, s), pairs + p))
    tspec = pl.BlockSpec((c, lanes), lambda b, p, s: (ctx_chunks + s, 0))
    vspec = pl.BlockSpec((c, 2 * RET_V_DIM), lambda b, p, s: (row(b, s), p))
    sspec = pl.BlockSpec((1, 1, 1, lanes, RET_V_DIM), lambda b, p, s: (b, p, ctx_chunks + s, 0, 0))
    gspec = pl.BlockSpec((1, 2 * RET_V_DIM), lambda b, p, s: (0, p))
    return pl.pallas_call(
        _ret_out_kernel,
        grid=(batch, pairs, x_chunks),
        in_specs=[pl.BlockSpec(memory_space=pltpu.SMEM), qspec, qspec, kspec, kspec, tspec, tspec,
                  vspec, vspec, sspec, sspec, gspec, gspec],
        out_specs=pl.BlockSpec((c, 2 * RET_V_DIM), lambda b, p, s: (b * x_chunks + s, p)),
        out_shape=jax.ShapeDtypeStruct((batch * x_chunks * c, vw), F32),
        compiler_params=_cparams(("parallel", "parallel", "parallel")),
        name="retention_out",
    )(decay_p, qk, qkp, qk, qkp, cos, sin, v, g, rf, rb, gn_g.reshape(1, vw), gn_b.reshape(1, vw))


def _conv_kernel(u_ref, up_ref, un_ref, w_ref, b_ref, lg_ref, lb_ref, o_ref, ext_ref, *, tiles_per_seq):
    j = pl.program_id(1)
    ch = o_ref.shape[-1]
    tm = u_ref.shape[0]
    glu = lambda t: t[:, :ch] * _sigmoid(t[:, ch:])
    has_prev = (j != 0).astype(F32)
    has_next = (j != tiles_per_seq - 1).astype(F32)
    ext_ref[0:HALO, :] = glu(up_ref[...]) * has_prev
    ext_ref[HALO:HALO + tm, :] = glu(u_ref[...])
    ext_ref[HALO + tm:, :] = glu(un_ref[...]) * has_next
    half = CONV_WIDTH // 2
    acc = jnp.zeros((tm, ch), F32) + b_ref[...]
    for t in range(CONV_WIDTH):
        acc = acc + w_ref[t:t + 1, :] * ext_ref[pl.ds(HALO - half + t, tm), :]
    mu = jnp.mean(acc, axis=-1, keepdims=True)
    dlt = acc - mu
    var = jnp.mean(dlt * dlt, axis=-1, keepdims=True)
    o_ref[...] = _silu(dlt * lax.rsqrt(var + LN_EPS) * lg_ref[...] + lb_ref[...])


def _conformer_conv(pu, batch, tiles_per_seq, ctx_tiles, conv_w, conv_b, ln_g, ln_b):
    rows, uw = pu.shape
    ch = uw // 2
    x_tiles = tiles_per_seq - ctx_tiles
    sub = ROW_TILE // HALO
    tile = lambda b, j: b * tiles_per_seq + ctx_tiles + j
    return pl.pallas_call(
        functools.partial(_conv_kernel, tiles_per_seq=x_tiles),
        grid=(batch, x_tiles),
        in_specs=[pl.BlockSpec((ROW_TILE, uw), lambda b, j: (tile(b, j), 0)),
                  pl.BlockSpec((HALO, uw), lambda b, j: (tile(b, j) * sub - 1, 0)),
                  pl.BlockSpec((HALO, uw), lambda b, j: (jnp.minimum((tile(b, j) + 1) * sub, rows // HALO - 1), 0)),
                  _const_spec((CONV_WIDTH, ch)), _const_spec((1, ch)), _const_spec((1, ch)), _const_spec((1, ch))],
        out_specs=pl.BlockSpec((ROW_TILE, ch), lambda b, j: (b * x_tiles + j, 0)),
        out_shape=jax.ShapeDtypeStruct((batch * x_tiles * ROW_TILE, ch), F32),
        scratch_shapes=[pltpu.VMEM((ROW_TILE + 2 * HALO, ch), F32)],
        compiler_params=_cparams(("parallel", "parallel")),
        name="conformer_conv",
    )(pu, pu, pu, conv_w, conv_b.reshape(1, ch), ln_g.reshape(1, ch), ln_b.reshape(1, ch))


def _rope_tables(n_ctx, n_tokens):
    rows = n_tokens // GRID_W
    row = jnp.repeat(jnp.arange(rows, dtype=F32), GRID_W)
    col = jnp.tile(jnp.arange(GRID_W, dtype=F32), rows)
    n_freq = RET_QK_DIM // 4
    freqs = ROPE_BASE ** (-jnp.arange(n_freq, dtype=F32) / n_freq)
    ang = jnp.concatenate([row[:, None] * freqs, col[:, None] * freqs], axis=-1)
    cos, sin = jnp.cos(ang), jnp.sin(ang)
    cos_h = jnp.concatenate([cos, cos], axis=-1)
    sin_h = jnp.concatenate([-sin, sin], axis=-1)
    cos_t = jnp.concatenate([jnp.ones((n_ctx, RET_QK_DIM), F32), cos_h], axis=0)
    sin_t = jnp.concatenate([jnp.zeros((n_ctx, RET_QK_DIM), F32), sin_h], axis=0)
    return jnp.tile(cos_t, (1, 2)), jnp.tile(sin_t, (1, 2))


def _swap_halves_cols(w, head_dim):
    d, n = w.shape
    w4 = w.reshape(d, n // head_dim, 2, head_dim // 2)
    return w4[:, :, ::-1, :].reshape(d, n)


def kernel(x, c, ctx, c_ctx, ada_w, ada_b, norm_mix_g, norm_ffn_g, ab_w_in, ab_mu_prev, ab_mu_next, ab_w0, ab_w2,
           ab_a0, ab_a2, ab_g2, ab_k_k, ab_k_a, ab_r_k, ab_lnx_g, ab_lnx_b, ab_w_out, cd_w_in, cd_decay_p, cd_gn_g,
           cd_gn_b, cd_conv_w, cd_conv_b, cd_ln_g, cd_ln_b, cd_w_out, ffn_w1, ffn_w3, ffn_w2, final_g):
    batch, seq, d = x.shape
    n_ctx = ctx.shape[1]
    depth = ada_w.shape[0]
    assert depth == 2 and n_ctx % ROW_TILE == 0 and seq % ROW_TILE == 0
    t_all = n_ctx + seq
    tiles_per_seq = t_all // ROW_TILE
    ctx_tiles = n_ctx // ROW_TILE
    x_tiles = tiles_per_seq - ctx_tiles
    n_tiles = batch * tiles_per_seq

    pad = (-(batch + 1)) % 8
    cpad = jnp.concatenate([c, c_ctx[None, :], jnp.zeros((pad, d), F32)], axis=0)
    mods = _mods(cpad, ada_w, ada_b)
    ctx_rows = jnp.broadcast_to(mods[:, batch:batch + 1], (depth, batch, 6 * d))
    mods_tab = jnp.stack([ctx_rows, mods[:, :batch]], axis=2).reshape(depth, 2 * batch, 1, 6 * d)
    mod_row = lambda i: 2 * (i // tiles_per_seq) + ((i % tiles_per_seq) >= ctx_tiles).astype(jnp.int32)

    stream = jnp.concatenate([ctx, x], axis=1).reshape(batch * t_all, d)

    fw = ab_w_in.shape[2] - ab_mu_prev.shape[1]
    pf, pr = _nm_matmul(stream, mods_tab[0], mod_row, norm_mix_g[0], ab_w_in[0].astype(BF16),
                        (fw, ab_mu_prev.shape[1]))
    pf3 = pf.reshape(batch, t_all, fw)
    ff = jnp.concatenate([_fourier(pf3[:, :n_ctx]), _fourier(pf3[:, n_ctx:])], axis=1).reshape(batch * t_all, fw)
    r, v, kk, g, bonus, lw, kd, bb = _rwkv_features(
        pr, tiles_per_seq, ctx_tiles, ab_mu_prev[0], ab_mu_next[0], ab_w0[0], ab_w2[0], ab_a0[0], ab_a2[0],
        ab_g2[0], ab_k_k[0], ab_k_a[0], ab_r_k[0])
    yf, yb = _rwkv_chunks(r, v, kk, lw, kd, bb, batch, t_all // RWKV_CHUNK, n_ctx // RWKV_CHUNK)
    rw = _rwkv_out(yf, yb, bonus, g, ab_lnx_g[0], ab_lnx_b[0])
    ident = lambda i: i
    stream = _outproj_ffn(stream, n_tiles, ident, [ff, rw], ident, ab_w_out[0].astype(BF16), mods_tab[0], mod_row,
                          norm_ffn_g[0], ffn_w1[0].astype(BF16), ffn_w3[0].astype(BF16), ffn_w2[0].astype(BF16),
                          final_g, False)

    w_in = cd_w_in[0]
    qk_w = 2 * (cd_gn_g.shape[1] // RET_V_DIM) * RET_QK_DIM
    vw = cd_gn_g.shape[1]
    w_ext = jnp.concatenate([w_in[:, :qk_w], _swap_halves_cols(w_in[:, :qk_w], RET_QK_DIM), w_in[:, qk_w:]], axis=1)
    uw = w_in.shape[1] - qk_w - 2 * vw
    qk, qkp, v, g, pu = _nm_matmul(stream, mods_tab[1], mod_row, norm_mix_g[1], w_ext.astype(BF16),
                                   (qk_w, qk_w, vw, vw, uw))
    cos, sin = _rope_tables(n_ctx, seq)
    n_chunks = t_all // RET_CHUNK
    ctx_chunks = n_ctx // RET_CHUNK
    rf, rb = _ret_states(cd_decay_p[0], qk, qkp, cos, sin, v, batch, n_chunks, ctx_chunks)
    ret = _ret_out(cd_decay_p[0], qk, qkp, cos, sin, v, g, rf, rb, cd_gn_g[0], cd_gn_b[0], batch, n_chunks, ctx_chunks)
    cv = _conformer_conv(pu, batch, tiles_per_seq, ctx_tiles, cd_conv_w[0], cd_conv_b[0], cd_ln_g[0], cd_ln_b[0])
    lat_tile = lambda i: (i // x_tiles) * tiles_per_seq + ctx_tiles + (i % x_tiles)
    mod_row_x = lambda i: 2 * (i // x_tiles) + 1
    out = _outproj_ffn(stream, batch * x_tiles, lat_tile, [ret, cv], ident, cd_w_out[0].astype(BF16), mods_tab[1],
                       mod_row_x, norm_ffn_g[1], ffn_w1[1].astype(BF16), ffn_w3[1].astype(BF16),
                       ffn_w2[1].astype(BF16), final_g, True)
    return out.reshape(batch, seq, d)
```
